```python
import math
import jax, jax.numpy as jnp
from jax import lax
import numpy as np

D_MODEL = 4096
BATCH = 2
SEQ = 8192
DEPTH = 2

MIX_WIDTH = D_MODEL
ATTN_WIDTH = MIX_WIDTH // 2
CONV_WIDTH = MIX_WIDTH - ATTN_WIDTH
ATTN_HEAD_DIM = 128
N_ATTN_HEADS = ATTN_WIDTH // (2 * ATTN_HEAD_DIM)
QK_WIDTH = N_ATTN_HEADS * 2 * ATTN_HEAD_DIM
W_IN_EVEN = 2 * QK_WIDTH + ATTN_WIDTH + 2 * CONV_WIDTH
CONF_KERNEL = 31
SHORT_KERNEL = 3
D_FF = 256 * ((8 * D_MODEL // 3 + 255) // 256)
FFN_KERNEL = 3
N_BUCKETS = 32
MAX_DISTANCE = 128
Q_BLOCK = 128
EPS = 1e-6
N_EVEN = (DEPTH + 1) // 2
N_ODD = DEPTH // 2

kernel_name = 'hybrid_diffattn_conformer_shortconv_encoder'


def rms_norm(x, g):
    xf = x.astype(jnp.float32)
    y = xf * lax.rsqrt(jnp.mean(xf * xf, axis=-1, keepdims=True) + EPS)
    return (y * g.astype(jnp.float32)).astype(x.dtype)


def layer_norm(x, g, b):
    xf = x.astype(jnp.float32)
    mu = jnp.mean(xf, axis=-1, keepdims=True)
    var = jnp.mean(jnp.square(xf - mu), axis=-1, keepdims=True)
    y = (xf - mu) * lax.rsqrt(var + EPS)
    return (y * g.astype(jnp.float32) + b.astype(jnp.float32)).astype(x.dtype)


def depthwise_conv(x, w):
    k, c = w.shape
    pad = (k - 1) // 2
    return lax.conv_general_dilated(
        x, w[:, None, :], window_strides=(1,), padding=[(pad, pad)],
        dimension_numbers=('NWC', 'WIO', 'NWC'), feature_group_count=c)


def t5_bucket(rel):
    nb = N_BUCKETS // 2
    max_exact = nb // 2
    ret = jnp.where(rel > 0, nb, 0)
    n = jnp.abs(rel)
    nf = jnp.maximum(n, max_exact).astype(jnp.float32)
    large = max_exact + (jnp.log(nf / max_exact) / math.log(MAX_DISTANCE / max_exact)
                         * (nb - max_exact)).astype(jnp.int32)
    large = jnp.minimum(large, nb - 1)
    return ret + jnp.where(n < max_exact, n, large)


def diff_attention(q, k, v, rel_bias, lam, lam_init, subln_g):
    b, s, h, _, d = q.shape
    scale = d ** -0.5
    nblk = s // Q_BLOCK
    qb = q.reshape(b, nblk, Q_BLOCK, h, 2, d).transpose(1, 0, 2, 3, 4, 5)
    kpos = jnp.arange(s, dtype=jnp.int32)

    def block(args):
        q_blk, start = args
        qpos = start + jnp.arange(Q_BLOCK, dtype=jnp.int32)
        bias = rel_bias[t5_bucket(kpos[None, :] - qpos[:, None])]
        bias = bias.transpose(2, 0, 1).astype(jnp.float32)
        logits = jnp.einsum('bqhmd,bkhmd->bhmqk', q_blk, k,
                            preferred_element_type=jnp.float32) * scale
        p = jax.nn.softmax(logits + bias[None, :, None], axis=-1)
        w = p[:, :, 0] - lam * p[:, :, 1]
        return jnp.einsum('bhqk,bkhe->bqhe', w.astype(v.dtype), v)

    starts = jnp.arange(nblk, dtype=jnp.int32) * Q_BLOCK
    out = lax.map(block, (qb, starts))
    out = out.transpose(1, 0, 2, 3, 4).reshape(b, s, h, 2 * d)
    out = rms_norm(out, subln_g) * (1.0 - lam_init)
    return out.reshape(b, s, h * 2 * d)


def even_mixer(hn, w_in, lq1, lk1, lq2, lk2, subln_g, rel_bias,
               conf_w, conf_b, conf_ln_g, conf_ln_b, w_out, lam_init):
    b, s, _ = hn.shape
    proj = hn @ w_in
    q, k, v, cv, cg = jnp.split(
        proj, [QK_WIDTH, 2 * QK_WIDTH, 2 * QK_WIDTH + ATTN_WIDTH,
               2 * QK_WIDTH + ATTN_WIDTH + CONV_WIDTH], axis=-1)
    q = q.reshape(b, s, N_ATTN_HEADS, 2, ATTN_HEAD_DIM)
    k = k.reshape(b, s, N_ATTN_HEADS, 2, ATTN_HEAD_DIM)
    v = v.reshape(b, s, N_ATTN_HEADS, 2 * ATTN_HEAD_DIM)
    lam = (jnp.exp(jnp.sum(lq1.astype(jnp.float32) * lk1.astype(jnp.float32)))
           - jnp.exp(jnp.sum(lq2.astype(jnp.float32) * lk2.astype(jnp.float32)))
           + lam_init)
    attn = diff_attention(q, k, v, rel_bias, lam, lam_init, subln_g)
    u = cv * jax.nn.sigmoid(cg)
    u = depthwise_conv(u, conf_w) + conf_b
    u = jax.nn.silu(layer_norm(u, conf_ln_g, conf_ln_b))
    return jnp.concatenate([attn, u], axis=-1) @ w_out


def odd_mixer(hn, w_in, conv_w, w_out):
    g_b, g_c, xv = jnp.split(hn @ w_in, 3, axis=-1)
    y = g_b * depthwise_conv(g_c * xv, conv_w)
    return y @ w_out


def conv_ffn(hn, w_gate, w_up, conv_w, conv_b, w_down):
    g = depthwise_conv(hn @ w_gate, conv_w) + conv_b
    return (jax.nn.gelu(g, approximate=True) * (hn @ w_up)) @ w_down


def setup_inputs(seed: int = 0) -> dict:
    key = jax.random.key(seed)
    ks = jax.random.split(key, 26)
    f32 = jnp.float32

    def nrm(k, shape, scale):
        return jax.random.normal(k, shape, f32) * scale

    def gain(k, shape):
        return 1.0 + 0.05 * jax.random.normal(k, shape, f32)

    return {
        'x': nrm(ks[0], (BATCH, SEQ, D_MODEL), 1.0),
        'rel_bias': nrm(ks[1], (N_BUCKETS, N_ATTN_HEADS), 0.5),
        'ev_w_in': nrm(ks[2], (N_EVEN, D_MODEL, W_IN_EVEN), D_MODEL ** -0.5),
        'ev_lambda_q1': nrm(ks[3], (N_EVEN, ATTN_HEAD_DIM), 0.1),
        'ev_lambda_k1': nrm(ks[4], (N_EVEN, ATTN_HEAD_DIM), 0.1),
        'ev_lambda_q2': nrm(ks[5], (N_EVEN, ATTN_HEAD_DIM), 0.1),
        'ev_lambda_k2': nrm(ks[6], (N_EVEN, ATTN_HEAD_DIM), 0.1),
        'ev_subln_g': gain(ks[7], (N_EVEN, 2 * ATTN_HEAD_DIM)),
        'ev_conf_w': nrm(ks[8], (N_EVEN, CONF_KERNEL, CONV_WIDTH), CONF_KERNEL ** -0.5),
        'ev_conf_b': nrm(ks[9], (N_EVEN, CONV_WIDTH), 0.02),
        'ev_conf_ln_g': gain(ks[10], (N_EVEN, CONV_WIDTH)),
        'ev_conf_ln_b': nrm(ks[11], (N_EVEN, CONV_WIDTH), 0.02),
        'ev_w_out': nrm(ks[12], (N_EVEN, MIX_WIDTH, D_MODEL), MIX_WIDTH ** -0.5),
        'od_w_in': nrm(ks[13], (N_ODD, D_MODEL, 3 * D_MODEL), D_MODEL ** -0.5),
        'od_conv_w': nrm(ks[14], (N_ODD, SHORT_KERNEL, D_MODEL), SHORT_KERNEL ** -0.5),
        'od_w_out': nrm(ks[15], (N_ODD, D_MODEL, D_MODEL), D_MODEL ** -0.5),
        'ffn_w_gate': nrm(ks[16], (DEPTH, D_MODEL, D_FF), D_MODEL ** -0.5),
        'ffn_w_up': nrm(ks[17], (DEPTH, D_MODEL, D_FF), D_MODEL ** -0.5),
        'ffn_conv_w': nrm(ks[18], (DEPTH, FFN_KERNEL, D_FF), FFN_KERNEL ** -0.5),
        'ffn_conv_b': nrm(ks[19], (DEPTH, D_FF), 0.02),
        'ffn_w_down': nrm(ks[20], (DEPTH, D_FF, D_MODEL), D_FF ** -0.5),
        'pre_mix_g': gain(ks[21], (DEPTH, D_MODEL)),
        'post_mix_g': gain(ks[22], (DEPTH, D_MODEL)),
        'pre_ffn_g': gain(ks[23], (DEPTH, D_MODEL)),
        'post_ffn_g': gain(ks[24], (DEPTH, D_MODEL)),
    }


def reference(x, rel_bias, ev_w_in, ev_lambda_q1, ev_lambda_k1, ev_lambda_q2, ev_lambda_k2,
              ev_subln_g, ev_conf_w, ev_conf_b, ev_conf_ln_g, ev_conf_ln_b, ev_w_out,
              od_w_in, od_conv_w, od_w_out, ffn_w_gate, ffn_w_up, ffn_conv_w, ffn_conv_b,
              ffn_w_down, pre_mix_g, post_mix_g, pre_ffn_g, post_ffn_g):
    for i in range(DEPTH):
        j = i // 2
        hn = rms_norm(x, pre_mix_g[i])
        if i % 2 == 0:
            lam_init = 0.8 - 0.6 * math.exp(-0.3 * i)
            m = even_mixer(hn, ev_w_in[j], ev_lambda_q1[j], ev_lambda_k1[j],
                           ev_lambda_q2[j], ev_lambda_k2[j], ev_subln_g[j], rel_bias,
                           ev_conf_w[j], ev_conf_b[j], ev_conf_ln_g[j], ev_conf_ln_b[j],
                           ev_w_out[j], lam_init)
        else:
            m = odd_mixer(hn, od_w_in[j], od_conv_w[j], od_w_out[j])
        x = x + rms_norm(m, post_mix_g[i])
        hn = rms_norm(x, pre_ffn_g[i])
        f = conv_ffn(hn, ffn_w_gate[i], ffn_w_up[i], ffn_conv_w[i], ffn_conv_b[i], ffn_w_down[i])
        x = x + rms_norm(f, post_ffn_g[i])
    return x
```

```python
import functools
import math

import jax
import jax.numpy as jnp
from jax import lax
from jax.experimental import pallas as pl
from jax.experimental.pallas import tpu as pltpu

F32 = jnp.float32
BF16 = jnp.bfloat16

EPS = 1e-6
HEAD_DIM = 128
N_BUCKETS = 32
MAX_DISTANCE = 128
HALO = 16
VMEM_LIMIT_V7X = 56 * 1024 * 1024


def _tile(n, pref, align):
    if n <= pref:
        return n
    t = (pref // align) * align
    while t >= align:
        if n % t == 0:
            return t
        t -= align
    raise ValueError(f"no tile for {n} (pref {pref}, align {align})")


def _params(sem, vmem_bytes):
    return pltpu.CompilerParams(dimension_semantics=sem,
                                vmem_limit_bytes=min(int(vmem_bytes), VMEM_LIMIT_V7X))


def _rms(v, g):
    return v * lax.rsqrt(jnp.mean(v * v, axis=-1, keepdims=True) + EPS) * g


def _norm_cast_kernel(x_ref, g_ref, o_ref):
    o_ref[...] = _rms(x_ref[...], g_ref[...]).astype(o_ref.dtype)


def norm_cast(x, g, *, tr=256):
    m, d = x.shape
    tr = _tile(m, tr, 16)
    return pl.pallas_call(
        _norm_cast_kernel,
        grid=(m // tr,),
        in_specs=[pl.BlockSpec((tr, d), lambda i: (i, 0)),
                  pl.BlockSpec((1, d), lambda i: (0, 0))],
        out_specs=pl.BlockSpec((tr, d), lambda i: (i, 0)),
        out_shape=jax.ShapeDtypeStruct((m, d), BF16),
        compiler_params=_params(("parallel",), 2 * tr * d * 6 + (4 << 20)),
        name="norm_cast",
    )(x, g.reshape(1, d))


def _residual_kernel(x_ref, m_ref, gpost_ref, *rest, with_next):
    if with_next:
        gpre_ref, xo_ref, hn_ref = rest
    else:
        (xo_ref,) = rest
    xn = x_ref[...] + _rms(m_ref[...], gpost_ref[...])
    xo_ref[...] = xn
    if with_next:
        hn_ref[...] = _rms(xn, gpre_ref[...]).astype(hn_ref.dtype)


def residual_norm(x, m_, g_post, g_pre=None, *, tr=256):
    m, d = x.shape
    tr = _tile(m, tr, 16)
    with_next = g_pre is not None
    row = pl.BlockSpec((tr, d), lambda i: (i, 0))
    vec = pl.BlockSpec((1, d), lambda i: (0, 0))
    in_specs = [row, row, vec] + ([vec] if with_next else [])
    args = [x, m_, g_post.reshape(1, d)] + ([g_pre.reshape(1, d)] if with_next else [])
    out_shape = [jax.ShapeDtypeStruct((m, d), F32)]
    out_specs = [row]
    if with_next:
        out_shape.append(jax.ShapeDtypeStruct((m, d), BF16))
        out_specs.append(row)
    outs = pl.pallas_call(
        functools.partial(_residual_kernel, with_next=with_next),
        grid=(m // tr,),
        in_specs=in_specs,
        out_specs=out_specs,
        out_shape=out_shape,
        compiler_params=_params(("parallel",), 2 * tr * d * 14 + (8 << 20)),
        name="residual_norm",
    )(*args)
    return (outs[0], outs[1]) if with_next else (outs[0], None)


def _matmul_kernel(*refs, k_sizes, k_steps, has_scale):
    n = len(k_sizes)
    x_refs, w_ref = refs[:n], refs[n]
    pos = n + 1
    scale_ref = refs[pos] if has_scale else None
    pos += int(has_scale)
    o_ref = refs[pos]
    acc = None
    off = 0
    for x_ref, ks in zip(x_refs, k_sizes):
        part = jnp.dot(x_ref[...], w_ref[off:off + ks, :], preferred_element_type=F32)
        acc = part if acc is None else acc + part
        off += ks
    if k_steps == 1:
        if has_scale:
            acc = acc * scale_ref[...]
        o_ref[...] = acc.astype(o_ref.dtype)
        return
    acc_ref = refs[pos + 1]
    k = pl.program_id(2)

    @pl.when(k == 0)
    def _():
        acc_ref[...] = acc

    @pl.when(jnp.logical_and(k > 0, k < k_steps - 1))
    def _():
        acc_ref[...] += acc

    @pl.when(k == k_steps - 1)
    def _():
        o_ref[...] = (acc_ref[...] + acc).astype(o_ref.dtype)


def matmul(xs, w, out_dtype, *, col_scale=None, tm=1024, tn=1024, k_steps=1):
    m = xs[0].shape[0]
    k_sizes = tuple(x.shape[1] for x in xs)
    k_total, n = w.shape
    assert sum(k_sizes) == k_total
    assert k_steps == 1 or (len(xs) == 1 and col_scale is None and k_steps >= 2)
    tm = _tile(m, tm, 16)
    tn = _tile(n, tn, 128)
    assert k_total % k_steps == 0
    tk = k_total // k_steps
    if k_steps == 1:
        grid = (m // tm, n // tn)
        x_specs = [pl.BlockSpec((tm, ks), lambda i, j: (i, 0)) for ks in k_sizes]
        w_spec = pl.BlockSpec((k_total, tn), lambda i, j: (0, j))
        s_spec = pl.BlockSpec((1, tn), lambda i, j: (0, j))
        o_spec = pl.BlockSpec((tm, tn), lambda i, j: (i, j))
        scratch = []
        sem = ("parallel", "arbitrary")
        blk_k = k_sizes
    else:
        grid = (m // tm, n // tn, k_steps)
        x_specs = [pl.BlockSpec((tm, tk), lambda i, j, k: (i, k))]
        w_spec = pl.BlockSpec((tk, tn), lambda i, j, k: (k, j))
        s_spec = None
        o_spec = pl.BlockSpec((tm, tn), lambda i, j, k: (i, j))
        scratch = [pltpu.VMEM((tm, tn), F32)]
        sem = ("parallel", "arbitrary", "arbitrary")
        blk_k = (tk,)
    args = list(xs) + [w]
    in_specs = x_specs + [w_spec]
    if col_scale is not None:
        args.append(col_scale.reshape(1, n).astype(F32))
        in_specs.append(s_spec)
    out_bytes = jnp.dtype(out_dtype).itemsize
    vmem = (2 * (tm * tk * 2 + tk * tn * 2 + tm * tn * out_bytes)
            + 3 * tm * tn * 4 + (4 << 20))
    return pl.pallas_call(
        functools.partial(_matmul_kernel, k_sizes=blk_k, k_steps=k_steps,
                          has_scale=col_scale is not None),
        grid=grid,
        in_specs=in_specs,
        out_specs=o_spec,
        out_shape=jax.ShapeDtypeStruct((m, n), out_dtype),
        scratch_shapes=scratch,
        compiler_params=_params(sem, vmem),
        name="matmul",
    )(*args)


def _halo_specs(tm, width, col_block, m_rows, n_grid_axes):
    per = tm // HALO
    last = m_rows // HALO - 1

    def prev_map(i, *_):
        return (jnp.maximum(i * per - 1, 0), col_block)

    def main_map(i, *_):
        return (i, col_block)

    def next_map(i, *_):
        return (jnp.minimum((i + 1) * per, last), col_block)

    del n_grid_axes
    return [pl.BlockSpec((HALO, width), prev_map),
            pl.BlockSpec((tm, width), main_map),
            pl.BlockSpec((HALO, width), next_map)]


def _seq_edges(i, tiles_per_seq):
    r = i % tiles_per_seq
    return r == 0, r == tiles_per_seq - 1


def _fill_ext(ext_ref, prev, main, nxt, first, last, tm):
    ext_ref[HALO:HALO + tm, :] = main
    ext_ref[0:HALO, :] = jnp.where(first, jnp.zeros_like(prev), prev)
    ext_ref[HALO + tm:HALO + tm + HALO, :] = jnp.where(last, jnp.zeros_like(nxt), nxt)


def _conv3(ext_ref, w_ref, tm):
    return (w_ref[0:1, :] * ext_ref[HALO - 1:HALO - 1 + tm, :]
            + w_ref[1:2, :] * ext_ref[HALO:HALO + tm, :]
            + w_ref[2:3, :] * ext_ref[HALO + 1:HALO + 1 + tm, :])


def _gelu_tanh(v):
    c = math.sqrt(2.0 / math.pi)
    return 0.5 * v * (1.0 + jnp.tanh(c * (v + 0.044715 * (v * v * v))))


def _ffn_up_kernel(xp_ref, x_ref, xn_ref, wg_ref, wu_ref, cw_ref, cb_ref, o_ref,
                   xext_ref, gext_ref, *, tm, tiles_per_seq):
    i = pl.program_id(0)

    @pl.when(pl.program_id(1) == 0)
    def _():
        first, last = _seq_edges(i, tiles_per_seq)
        _fill_ext(xext_ref, xp_ref[...], x_ref[...], xn_ref[...], first, last, tm)

    gext_ref[...] = jnp.dot(xext_ref[...], wg_ref[...], preferred_element_type=F32)
    up = jnp.dot(x_ref[...], wu_ref[...], preferred_element_type=F32)
    g = _conv3(gext_ref, cw_ref, tm) + cb_ref[...]
    o_ref[...] = (_gelu_tanh(g) * up).astype(o_ref.dtype)


def ffn_up(hn, w_gate, w_up, conv_w, conv_b, seq, *, tm=1024, tn=256):
    m, d = hn.shape
    f = w_gate.shape[1]
    tm = _tile(seq, tm, HALO)
    tn = _tile(f, tn, 128)
    wcol = pl.BlockSpec((d, tn), lambda i, j: (0, j))
    vmem = (2 * (tm * d * 2 + 2 * d * tn * 2 + tm * tn * 2) + (tm + 2 * HALO) * d * 2
            + 6 * (tm + 2 * HALO) * tn * 4 + (4 << 20))
    return pl.pallas_call(
        functools.partial(_ffn_up_kernel, tm=tm, tiles_per_seq=seq // tm),
        grid=(m // tm, f // tn),
        in_specs=_halo_specs(tm, d, 0, m, 2) + [
            wcol, wcol,
            pl.BlockSpec((3, tn), lambda i, j: (0, j)),
            pl.BlockSpec((1, tn), lambda i, j: (0, j))],
        out_specs=pl.BlockSpec((tm, tn), lambda i, j: (i, j)),
        out_shape=jax.ShapeDtypeStruct((m, f), BF16),
        scratch_shapes=[pltpu.VMEM((tm + 2 * HALO, d), BF16),
                        pltpu.VMEM((tm + 2 * HALO, tn), F32)],
        compiler_params=_params(("parallel", "arbitrary"), vmem),
        name="ffn_up",
    )(hn, hn, hn, w_gate, w_up, conv_w, conv_b.reshape(1, f))


def _short_mixer_kernel(xp_ref, x_ref, xn_ref, wb_ref, wc_ref, wx_ref, cw_ref, o_ref,
                        xext_ref, pext_ref, *, tm, tiles_per_seq):
    i = pl.program_id(0)

    @pl.when(pl.program_id(1) == 0)
    def _():
        first, last = _seq_edges(i, tiles_per_seq)
        _fill_ext(xext_ref, xp_ref[...], x_ref[...], xn_ref[...], first, last, tm)

    xe = xext_ref[...]
    gc = jnp.dot(xe, wc_ref[...], preferred_element_type=F32)
    xv = jnp.dot(xe, wx_ref[...], preferred_element_type=F32)
    pext_ref[...] = gc * xv
    gb = jnp.dot(x_ref[...], wb_ref[...], preferred_element_type=F32)
    o_ref[...] = (gb * _conv3(pext_ref, cw_ref, tm)).astype(o_ref.dtype)


def short_mixer(hn, w_in, conv_w, seq, *, tm=1024, tn=256):
    m, d = hn.shape
    c = w_in.shape[1] // 3
    tm = _tile(seq, tm, HALO)
    tn = _tile(c, tn, 128)
    nb = c // tn
    vmem = (2 * (tm * d * 2 + 3 * d * tn * 2 + tm * tn * 2) + (tm + 2 * HALO) * d * 2
            + 7 * (tm + 2 * HALO) * tn * 4 + (4 << 20))
    return pl.pallas_call(
        functools.partial(_short_mixer_kernel, tm=tm, tiles_per_seq=seq // tm),
        grid=(m // tm, nb),
        in_specs=_halo_specs(tm, d, 0, m, 2) + [
            pl.BlockSpec((d, tn), lambda i, j: (0, j)),
            pl.BlockSpec((d, tn), lambda i, j: (0, nb + j)),
            pl.BlockSpec((d, tn), lambda i, j: (0, 2 * nb + j)),
            pl.BlockSpec((3, tn), lambda i, j: (0, j))],
        out_specs=pl.BlockSpec((tm, tn), lambda i, j: (i, j)),
        out_shape=jax.ShapeDtypeStruct((m, c), BF16),
        scratch_shapes=[pltpu.VMEM((tm + 2 * HALO, d), BF16),
                        pltpu.VMEM((tm + 2 * HALO, tn), F32)],
        compiler_params=_params(("parallel", "arbitrary"), vmem),
        name="short_mixer",
    )(hn, hn, hn, w_in, w_in, w_in, conv_w)


def _conformer_kernel(vp_ref, v_ref, vn_ref, gp_ref, g_ref, gn_ref, w_ref, b_ref, lg_ref, lb_ref,
                      o_ref, uext_ref, y_ref, *, ts, tiles_per_seq, taps, rc, cc):
    first, last = _seq_edges(pl.program_id(0), tiles_per_seq)

    def glu(v, g):
        return v.astype(F32) * jax.nn.sigmoid(g.astype(F32))

    _fill_ext(uext_ref, glu(vp_ref[...], gp_ref[...]), glu(v_ref[...], g_ref[...]),
              glu(vn_ref[...], gn_ref[...]), first, last, ts)
    pad = (taps - 1) // 2
    width = y_ref.shape[1]

    def lane_chunk(ci, carry):
        c0 = pl.multiple_of(ci * cc, cc)
        for r0 in range(0, ts, rc):
            acc = jnp.zeros((rc, cc), F32)
            for t in range(taps):
                acc = acc + (w_ref[t:t + 1, pl.ds(c0, cc)]
                             * uext_ref[r0 + HALO - pad + t:r0 + HALO - pad + t + rc, pl.ds(c0, cc)])
            y_ref[r0:r0 + rc, pl.ds(c0, cc)] = acc + b_ref[:, pl.ds(c0, cc)]
        return carry

    lax.fori_loop(0, width // cc, lane_chunk, 0)
    y = y_ref[...]
    mu = jnp.mean(y, axis=-1, keepdims=True)
    yc = y - mu
    var = jnp.mean(yc * yc, axis=-1, keepdims=True)
    z = yc * lax.rsqrt(var + EPS) * lg_ref[...] + lb_ref[...]
    o_ref[...] = (z * jax.nn.sigmoid(z)).astype(o_ref.dtype)


def conformer(proj, v_block, g_block, width, conv_w, conv_b, ln_g, ln_b, seq, *, ts=256):
    m = proj.shape[0]
    taps = conv_w.shape[0]
    assert (taps - 1) // 2 <= HALO
    ts = _tile(seq, ts, HALO)
    rc = _tile(ts, 64, 8)
    cc = _tile(width, 256, 128)
    vec = pl.BlockSpec((1, width), lambda i: (0, 0))
    vmem = 2 * (2 * ts * width * 2 + ts * width * 2) + 8 * (ts + 2 * HALO) * width * 4 + (4 << 20)
    return pl.pallas_call(
        functools.partial(_conformer_kernel, ts=ts, tiles_per_seq=seq // ts, taps=taps, rc=rc, cc=cc),
        grid=(m // ts,),
        in_specs=(_halo_specs(ts, width, v_block, m, 1) + _halo_specs(ts, width, g_block, m, 1)
                  + [pl.BlockSpec((taps, width), lambda i: (0, 0)), vec, vec, vec]),
        out_specs=pl.BlockSpec((ts, width), lambda i: (i, 0)),
        out_shape=jax.ShapeDtypeStruct((m, width), BF16),
        scratch_shapes=[pltpu.VMEM((ts + 2 * HALO, width), F32),
                        pltpu.VMEM((ts, width), F32)],
        compiler_params=_params(("parallel",), vmem),
        name="conformer",
    )(proj, proj, proj, proj, proj, proj, conv_w, conv_b.reshape(1, width),
      ln_g.reshape(1, width), ln_b.reshape(1, width))


def _t5_bucket(rel):
    nb = N_BUCKETS // 2
    max_exact = nb // 2
    ret = jnp.where(rel > 0, nb, 0)
    n = jnp.abs(rel)
    nf = jnp.maximum(n, max_exact).astype(F32)
    large = max_exact + (jnp.log(nf / max_exact) / math.log(MAX_DISTANCE / max_exact)
                         * (nb - max_exact)).astype(jnp.int32)
    large = jnp.minimum(large, nb - 1)
    return ret + jnp.where(n < max_exact, n, large)


def _bias_tiles(rel_bias, t):
    assert t >= MAX_DISTANCE
    off = jnp.arange(-2, 3, dtype=jnp.int32)[:, None, None] * t
    a = jnp.arange(t, dtype=jnp.int32)
    rel = off + a[None, None, :] - a[None, :, None]
    return jnp.transpose(rel_bias[_t5_bucket(rel)], (3, 0, 1, 2)).astype(F32)


def _attn_kernel(q_ref, k_ref, v_ref, bias_ref, lq1_ref, lk1_ref, lq2_ref, lk2_ref, g_ref, o_ref,
                 *, t, n_kv, lam_init):
    i = pl.program_id(2)
    d = HEAD_DIM
    nt = (((1,), (1,)), ((), ()))

    def step(j, carry):
        r0 = pl.multiple_of(j * t, t)
        bias = bias_ref[jnp.clip(j - i, -2, 2) + 2]
        vj = v_ref[pl.ds(r0, t), :]
        new = []
        for mp in range(2):
            mx, l, acc = carry[3 * mp:3 * mp + 3]
            s = lax.dot_general(q_ref[:, mp * d:(mp + 1) * d], k_ref[pl.ds(r0, t), mp * d:(mp + 1) * d],
                                nt, preferred_element_type=F32) + bias
            mn = jnp.maximum(mx, jnp.max(s, axis=1, keepdims=True))
            a = jnp.exp(mx - mn)
            p = jnp.exp(s - mn)
            l = a * l + jnp.sum(p, axis=1, keepdims=True)
            acc = a * acc + jnp.dot(p.astype(BF16), vj, preferred_element_type=F32)
            new += [mn, l, acc]
        return tuple(new)

    init = (jnp.full((t, 1), -1e30, F32), jnp.zeros((t, 1), F32), jnp.zeros((t, 2 * d), F32)) * 2
    _, l0, acc0, _, l1, acc1 = lax.fori_loop(0, n_kv, step, init)
    lam = (jnp.exp(jnp.sum(lq1_ref[...] * lk1_ref[...], axis=-1, keepdims=True))
           - jnp.exp(jnp.sum(lq2_ref[...] * lk2_ref[...], axis=-1, keepdims=True)) + lam_init)
    o = acc0 / l0 - lam * (acc1 / l1)
    o_ref[...] = (_rms(o, g_ref[...]) * (1.0 - lam_init)).astype(o_ref.dtype)


def diff_attention(proj, rel_bias, lq1, lk1, lq2, lk2, subln_g, batch, seq, n_heads, lam_init, *, t=512):
    m = proj.shape[0]
    d2 = 2 * HEAD_DIM
    t = _tile(seq, t, MAX_DISTANCE)
    nq = seq // t
    bias = _bias_tiles(rel_bias, t)
    vec = pl.BlockSpec((1, HEAD_DIM), lambda b, h, i: (0, 0))
    vmem = 2 * (2 * seq * d2 * 2 + 5 * t * t * 4 + 2 * t * d2 * 2) + 12 * t * t * 4 + (4 << 20)
    return pl.pallas_call(
        functools.partial(_attn_kernel, t=t, n_kv=nq, lam_init=lam_init),
        grid=(batch, n_heads, nq),
        in_specs=[pl.BlockSpec((t, d2), lambda b, h, i: (b * nq + i, h)),
                  pl.BlockSpec((seq, d2), lambda b, h, i: (b, n_heads + h)),
                  pl.BlockSpec((seq, d2), lambda b, h, i: (b, 2 * n_heads + h)),
                  pl.BlockSpec((None, 5, t, t), lambda b, h, i: (h, 0, 0, 0)),
                  vec, vec, vec, vec,
                  pl.BlockSpec((1, d2), lambda b, h, i: (0, 0))],
        out_specs=pl.BlockSpec((t, d2), lambda b, h, i: (b * nq + i, h)),
        out_shape=jax.ShapeDtypeStruct((m, n_heads * d2), BF16),
        compiler_params=_params(("parallel", "parallel", "arbitrary"), vmem),
        name="diff_attention",
    )(proj, proj, proj, bias, lq1.reshape(1, -1), lk1.reshape(1, -1), lq2.reshape(1, -1),
      lk2.reshape(1, -1), subln_g.reshape(1, d2))


def kernel(x, rel_bias, ev_w_in, ev_lambda_q1, ev_lambda_k1, ev_lambda_q2, ev_lambda_k2, ev_subln_g, ev_conf_w, ev_conf_b, ev_conf_ln_g, ev_conf_ln_b, ev_w_out, od_w_in, od_conv_w, od_w_out, ffn_w_gate, ffn_w_up, ffn_conv_w, ffn_conv_b, ffn_w_down, pre_mix_g, post_mix_g, pre_ffn_g, post_ffn_g):
    batch, seq, d = x.shape
    depth = pre_mix_g.shape[0]
    d2 = 2 * HEAD_DIM
    attn_width = d // 2
    n_heads = attn_width // d2
    qk_width = n_heads * d2
    conv_width = d - attn_width
    assert qk_width == attn_width == conv_width
    xr = x.reshape(batch * seq, d)
    hn = norm_cast(xr, pre_mix_g[0])
    for i in range(depth):
        j = i // 2
        if i % 2 == 0:
            lam_init = 0.8 - 0.6 * math.exp(-0.3 * i)
            w_in = ev_w_in[j].astype(BF16)
            q_scale = jnp.concatenate([jnp.full((qk_width,), HEAD_DIM ** -0.5, F32),
                                       jnp.ones((w_in.shape[1] - qk_width,), F32)])
            proj = matmul([hn], w_in, BF16, col_scale=q_scale)
            attn = diff_attention(proj, rel_bias, ev_lambda_q1[j], ev_lambda_k1[j], ev_lambda_q2[j],
                                  ev_lambda_k2[j], ev_subln_g[j], batch, seq, n_heads, lam_init)
            u = conformer(proj, 3, 4, conv_width, ev_conf_w[j], ev_conf_b[j], ev_conf_ln_g[j],
                          ev_conf_ln_b[j], seq)
            mix = matmul([attn, u], ev_w_out[j].astype(BF16), F32)
        else:
            y = short_mixer(hn, od_w_in[j].astype(BF16), od_conv_w[j], seq)
            mix = matmul([y], od_w_out[j].astype(BF16), F32)
        xr, hn = residual_norm(xr, mix, post_mix_g[i], pre_ffn_g[i])
        h = ffn_up(hn, ffn_w_gate[i].astype(BF16), ffn_w_up[i].astype(BF16), ffn_conv_w[i],
                   ffn_conv_b[i], seq)
        f = matmul([h], ffn_w_down[i].astype(BF16), F32, tn=512, k_steps=2)
        xr, hn = residual_norm(xr, f, post_ffn_g[i], pre_mix_g[i + 1] if i + 1 < depth else None)
    return xr.reshape(batch, seq, d)
```

```python
import functools
import math

import jax
import jax.numpy as jnp
from jax import lax
from jax.experimental import pallas as pl
from jax.experimental.pallas import tpu as pltpu

F32 = jnp.float32
BF16 = jnp.bfloat16

EPS = 1e-6
HEAD_DIM = 128
N_BUCKETS = 32
MAX_DISTANCE = 128
LOG2E = 1.4426950408889634
LANES = 128
SUBLANES = 8
HALO = 16
CAST_BLOCK_BYTES = 12 * 1024 * 1024
VMEM_LIMIT_V7X = 56 * 1024 * 1024


def _tile(n, pref, align):
    if n <= pref:
        return n
    t = (pref // align) * align
    while t >= align:
        if n % t == 0:
            return t
        t -= align
    raise ValueError(f"no tile for {n} (pref {pref}, align {align})")


def _params(sem, vmem_bytes, flags=None):
    return pltpu.CompilerParams(dimension_semantics=sem, flags=flags,
                                vmem_limit_bytes=min(int(vmem_bytes), VMEM_LIMIT_V7X))


def _rms(v, g):
    return v * lax.rsqrt(jnp.mean(v * v, axis=-1, keepdims=True) + EPS) * g


def _norm_cast_kernel(x_ref, g_ref, o_ref):
    o_ref[...] = _rms(x_ref[...], g_ref[...]).astype(o_ref.dtype)


def norm_cast(x, g, *, tr=256):
    m, d = x.shape
    tr = _tile(m, tr, 16)
    return pl.pallas_call(
        _norm_cast_kernel,
        grid=(m // tr,),
        in_specs=[pl.BlockSpec((tr, d), lambda i: (i, 0)),
                  pl.BlockSpec((1, d), lambda i: (0, 0))],
        out_specs=pl.BlockSpec((tr, d), lambda i: (i, 0)),
        out_shape=jax.ShapeDtypeStruct((m, d), BF16),
        compiler_params=_params(("parallel",), 2 * tr * d * 6 + (4 << 20)),
        name="norm_cast",
    )(x, g.reshape(1, d))


def _residual_kernel(x_ref, m_ref, gpost_ref, *rest, with_next):
    if with_next:
        gpre_ref, xo_ref, hn_ref = rest
    else:
        (xo_ref,) = rest
    xn = x_ref[...] + _rms(m_ref[...].astype(F32), gpost_ref[...])
    xo_ref[...] = xn
    if with_next:
        hn_ref[...] = _rms(xn, gpre_ref[...]).astype(hn_ref.dtype)


def residual_norm(x, m_, g_post, g_pre=None, *, tr=256):
    m, d = x.shape
    tr = _tile(m, tr, 16)
    with_next = g_pre is not None
    row = pl.BlockSpec((tr, d), lambda i: (i, 0))
    vec = pl.BlockSpec((1, d), lambda i: (0, 0))
    in_specs = [row, row, vec] + ([vec] if with_next else [])
    args = [x, m_, g_post.reshape(1, d)] + ([g_pre.reshape(1, d)] if with_next else [])
    out_shape = [jax.ShapeDtypeStruct((m, d), F32)]
    out_specs = [row]
    if with_next:
        out_shape.append(jax.ShapeDtypeStruct((m, d), BF16))
        out_specs.append(row)
    outs = pl.pallas_call(
        functools.partial(_residual_kernel, with_next=with_next),
        grid=(m // tr,),
        in_specs=in_specs,
        out_specs=out_specs,
        out_shape=out_shape,
        compiler_params=_params(("parallel",), 2 * tr * d * 14 + (8 << 20)),
        name="residual_norm",
    )(*args)
    return (outs[0], outs[1]) if with_next else (outs[0], None)


def _cast_kernel(w_ref, o_ref):
    o_ref[...] = w_ref[...].astype(o_ref.dtype)


def cast_layer(w_stack, layer):
    _, k, n = w_stack.shape
    tr = _tile(k, max(CAST_BLOCK_BYTES // (n * 4), 16), 16)
    return pl.pallas_call(
        _cast_kernel,
        grid=(k // tr,),
        in_specs=[pl.BlockSpec((None, tr, n), lambda r: (layer, r, 0))],
        out_specs=pl.BlockSpec((tr, n), lambda r: (r, 0)),
        out_shape=jax.ShapeDtypeStruct((k, n), BF16),
        compiler_params=_params(("parallel",), 2 * tr * n * 6 + (4 << 20)),
        name="cast_layer",
    )(w_stack)


def _matmul_kernel(*refs, k_sizes, has_scale, n_side):
    n = len(k_sizes)
    x_refs, w_ref = refs[:n], refs[n]
    pos = n + 1
    scale_ref = refs[pos] if has_scale else None
    pos += int(has_scale)
    side_in = refs[pos:pos + n_side]
    o_ref = refs[pos + n_side]
    side_out = refs[pos + n_side + 1:pos + 2 * n_side + 1]
    acc = None
    off = 0
    for x_ref, ks in zip(x_refs, k_sizes):
        part = jnp.dot(x_ref[...], w_ref[off:off + ks, :], preferred_element_type=F32)
        acc = part if acc is None else acc + part
        off += ks
    if has_scale:
        acc = acc * scale_ref[...]
    o_ref[...] = acc.astype(o_ref.dtype)
    _run_side_casts(side_in, side_out)


def _slab_rows(k, n_steps):
    for rows in range(16, k + 1, 16):
        if k % rows == 0 and k // rows <= n_steps:
            return rows
    raise ValueError(f"no slab for {k} rows in {n_steps} steps")


def _side_cast_specs(side_casts, n_steps, step_of):
    in_specs, out_specs, out_shapes, vmem = [], [], [], 0
    for w_stack, layer in side_casts:
        _, ks, ns = w_stack.shape
        rows = _slab_rows(ks, n_steps)
        last = ks // rows - 1

        def slab(*idx, last=last):
            return jnp.minimum(step_of(*idx), last)

        in_specs.append(pl.BlockSpec((None, rows, ns),
                                     lambda *idx, layer=layer, slab=slab: (layer, slab(*idx), 0)))
        out_specs.append(pl.BlockSpec((rows, ns), lambda *idx, slab=slab: (slab(*idx), 0)))
        out_shapes.append(jax.ShapeDtypeStruct((ks, ns), BF16))
        vmem += 2 * rows * ns * 6
    return in_specs, out_specs, out_shapes, vmem


def _run_side_casts(side_in, side_out):
    for src, dst in zip(side_in, side_out):
        dst[...] = src[...].astype(dst.dtype)


def matmul(xs, w, out_dtype, *, col_scale=None, tm=1024, tn=1024, side_casts=()):
    m = xs[0].shape[0]
    k_sizes = tuple(x.shape[1] for x in xs)
    k_total, n = w.shape
    assert sum(k_sizes) == k_total
    tm = _tile(m, tm, 16)
    tn = _tile(n, tn, LANES)
    nj = n // tn
    args = list(xs) + [w]
    in_specs = ([pl.BlockSpec((tm, ks), lambda i, j: (i, 0)) for ks in k_sizes]
                + [pl.BlockSpec((k_total, tn), lambda i, j: (0, j))])
    if col_scale is not None:
        args.append(col_scale.reshape(1, n).astype(F32))
        in_specs.append(pl.BlockSpec((1, tn), lambda i, j: (0, j)))
    out_bytes = jnp.dtype(out_dtype).itemsize
    vmem = (2 * (tm * k_total * 2 + k_total * tn * 2 + tm * tn * out_bytes) + 3 * tm * tn * 4 + (4 << 20))
    side_in, side_out, side_shapes, side_vmem = _side_cast_specs(
        side_casts, (m // tm) * nj, lambda i, j: i * nj + j)
    args += [w_stack for w_stack, _ in side_casts]
    in_specs += side_in
    out_specs = [pl.BlockSpec((tm, tn), lambda i, j: (i, j))] + side_out
    out_shape = [jax.ShapeDtypeStruct((m, n), out_dtype)] + side_shapes
    vmem += side_vmem
    outs = pl.pallas_call(
        functools.partial(_matmul_kernel, k_sizes=k_sizes, has_scale=col_scale is not None,
                          n_side=len(side_casts)),
        grid=(m // tm, nj),
        in_specs=in_specs,
        out_specs=out_specs,
        out_shape=out_shape,
        compiler_params=_params(("arbitrary", "arbitrary"), vmem),
        name="matmul",
    )(*args)
    return (outs[0], list(outs[1:])) if side_casts else outs[0]


def _halo_specs(tm, width, col_block, m_rows, n_grid_axes):
    per = tm // HALO
    last = m_rows // HALO - 1

    def prev_map(i, *_):
        return (jnp.maximum(i * per - 1, 0), col_block)

    def main_map(i, *_):
        return (i, col_block)

    def next_map(i, *_):
        return (jnp.minimum((i + 1) * per, last), col_block)

    del n_grid_axes
    return [pl.BlockSpec((HALO, width), prev_map),
            pl.BlockSpec((tm, width), main_map),
            pl.BlockSpec((HALO, width), next_map)]


def _seq_edges(i, tiles_per_seq):
    r = i % tiles_per_seq
    return r == 0, r == tiles_per_seq - 1


def _fill_ext(ext_ref, prev, main, nxt, first, last, tm):
    ext_ref[HALO:HALO + tm, :] = main
    ext_ref[0:HALO, :] = jnp.where(first, jnp.zeros_like(prev), prev)
    ext_ref[HALO + tm:HALO + tm + HALO, :] = jnp.where(last, jnp.zeros_like(nxt), nxt)


def _conv3(ext_ref, w_ref, r0, rows):
    lo = HALO + r0
    return (w_ref[0:1, :] * ext_ref[lo - 1:lo - 1 + rows, :]
            + w_ref[1:2, :] * ext_ref[lo:lo + rows, :]
            + w_ref[2:3, :] * ext_ref[lo + 1:lo + 1 + rows, :])


def _ext_chunk(c, rows):
    return (0 if c == 0 else c * rows + 2 * HALO), (c + 1) * rows + 2 * HALO


def _chunked(n_chunks, matmuls, epilogue):
    matmuls(0)
    for c in range(1, n_chunks):
        matmuls(c)
        epilogue(c - 1)
    epilogue(n_chunks - 1)


def _gelu_tanh(v):
    c = math.sqrt(2.0 / math.pi)
    return 0.5 * v * (1.0 + jnp.tanh(c * (v + 0.044715 * (v * v * v))))


def _ffn_up_kernel(xp_ref, x_ref, xn_ref, wg_ref, wu_ref, cw_ref, cb_ref, *rest,
                   tm, tiles_per_seq, n_chunks, n_side):
    side_in, o_ref, side_out = rest[:n_side], rest[n_side], rest[n_side + 1:2 * n_side + 1]
    xext_ref, gext_ref, up_ref = rest[2 * n_side + 1:]
    _run_side_casts(side_in, side_out)
    i = pl.program_id(0)
    rows = tm // n_chunks

    @pl.when(pl.program_id(1) == 0)
    def _():
        first, last = _seq_edges(i, tiles_per_seq)
        _fill_ext(xext_ref, xp_ref[...], x_ref[...], xn_ref[...], first, last, tm)

    def matmuls(c):
        lo, hi = _ext_chunk(c, rows)
        gext_ref[lo:hi, :] = jnp.dot(xext_ref[lo:hi, :], wg_ref[...], preferred_element_type=F32)
        up_ref[c * rows:(c + 1) * rows, :] = jnp.dot(x_ref[c * rows:(c + 1) * rows, :], wu_ref[...],
                                                     preferred_element_type=F32)

    def epilogue(c):
        r0 = c * rows
        g = _conv3(gext_ref, cw_ref, r0, rows) + cb_ref[...]
        o_ref[r0:r0 + rows, :] = (_gelu_tanh(g) * up_ref[r0:r0 + rows, :]).astype(o_ref.dtype)

    _chunked(n_chunks, matmuls, epilogue)


def ffn_up(hn, w_gate, w_up, conv_w, conv_b, seq, *, tm=1024, tn=256, n_chunks=1, side_casts=()):
    m, d = hn.shape
    f = w_gate.shape[1]
    tm = _tile(seq, tm, HALO * n_chunks)
    tn = _tile(f, tn, LANES)
    nj = f // tn
    wcol = pl.BlockSpec((d, tn), lambda i, j: (0, j))
    side_in, side_out, side_shapes, side_vmem = _side_cast_specs(
        side_casts, (m // tm) * nj, lambda i, j: i * nj + j)
    vmem = (2 * (tm * d * 2 + 2 * d * tn * 2 + tm * tn * 2) + (tm + 2 * HALO) * d * 2
            + 6 * (tm + 2 * HALO) * tn * 4 + (4 << 20) + side_vmem)
    outs = pl.pallas_call(
        functools.partial(_ffn_up_kernel, tm=tm, tiles_per_seq=seq // tm, n_chunks=n_chunks,
                          n_side=len(side_casts)),
        grid=(m // tm, nj),
        in_specs=_halo_specs(tm, d, 0, m, 2) + [
            wcol, wcol,
            pl.BlockSpec((3, tn), lambda i, j: (0, j)),
            pl.BlockSpec((1, tn), lambda i, j: (0, j))] + side_in,
        out_specs=[pl.BlockSpec((tm, tn), lambda i, j: (i, j))] + side_out,
        out_shape=[jax.ShapeDtypeStruct((m, f), BF16)] + side_shapes,
        scratch_shapes=[pltpu.VMEM((tm + 2 * HALO, d), BF16),
                        pltpu.VMEM((tm + 2 * HALO, tn), F32),
                        pltpu.VMEM((tm, tn), F32)],
        compiler_params=_params(("arbitrary", "arbitrary"), vmem),
        name="ffn_up",
    )(hn, hn, hn, w_gate, w_up, conv_w, conv_b.reshape(1, f), *[w_stack for w_stack, _ in side_casts])
    return outs[0], list(outs[1:])


def _short_mixer_kernel(xp_ref, x_ref, xn_ref, wb_ref, wc_ref, wx_ref, cw_ref, o_ref,
                        xext_ref, pext_ref, gb_ref, *, tm, tiles_per_seq, n_chunks):
    i = pl.program_id(0)
    rows = tm // n_chunks

    @pl.when(pl.program_id(1) == 0)
    def _():
        first, last = _seq_edges(i, tiles_per_seq)
        _fill_ext(xext_ref, xp_ref[...], x_ref[...], xn_ref[...], first, last, tm)

    def matmuls(c):
        lo, hi = _ext_chunk(c, rows)
        xe = xext_ref[lo:hi, :]
        pext_ref[lo:hi, :] = (jnp.dot(xe, wc_ref[...], preferred_element_type=F32)
                              * jnp.dot(xe, wx_ref[...], preferred_element_type=F32))
        gb_ref[c * rows:(c + 1) * rows, :] = jnp.dot(x_ref[c * rows:(c + 1) * rows, :], wb_ref[...],
                                                     preferred_element_type=F32)

    def epilogue(c):
        r0 = c * rows
        o_ref[r0:r0 + rows, :] = (gb_ref[r0:r0 + rows, :]
                                  * _conv3(pext_ref, cw_ref, r0, rows)).astype(o_ref.dtype)

    _chunked(n_chunks, matmuls, epilogue)


def short_mixer(hn, w_in, conv_w, seq, *, tm=1024, tn=256, n_chunks=8):
    m, d = hn.shape
    c = w_in.shape[1] // 3
    tm = _tile(seq, tm, HALO * n_chunks)
    tn = _tile(c, tn, LANES)
    nb = c // tn
    vmem = (2 * (tm * d * 2 + 3 * d * tn * 2 + tm * tn * 2) + (tm + 2 * HALO) * d * 2
            + 7 * (tm + 2 * HALO) * tn * 4 + (4 << 20))
    return pl.pallas_call(
        functools.partial(_short_mixer_kernel, tm=tm, tiles_per_seq=seq // tm, n_chunks=n_chunks),
        grid=(m // tm, nb),
        in_specs=_halo_specs(tm, d, 0, m, 2) + [
            pl.BlockSpec((d, tn), lambda i, j: (0, j)),
            pl.BlockSpec((d, tn), lambda i, j: (0, nb + j)),
            pl.BlockSpec((d, tn), lambda i, j: (0, 2 * nb + j)),
            pl.BlockSpec((3, tn), lambda i, j: (0, j))],
        out_specs=pl.BlockSpec((tm, tn), lambda i, j: (i, j)),
        out_shape=jax.ShapeDtypeStruct((m, c), BF16),
        scratch_shapes=[pltpu.VMEM((tm + 2 * HALO, d), BF16),
                        pltpu.VMEM((tm + 2 * HALO, tn), F32),
                        pltpu.VMEM((tm, tn), F32)],
        compiler_params=_params(("parallel", "arbitrary"), vmem),
        name="short_mixer",
    )(hn, hn, hn, w_in, w_in, w_in, conv_w)


def _conformer_kernel(vp_ref, v_ref, vn_ref, gp_ref, g_ref, gn_ref, w_ref, b_ref, lg_ref, lb_ref,
                      o_ref, shift_ref, y_ref, *, ts, tiles_per_seq, taps, rc, cc):
    first, last = _seq_edges(pl.program_id(0), tiles_per_seq)

    def glu(v, g):
        return v.astype(F32) * jax.nn.sigmoid(g.astype(F32))

    _fill_ext(shift_ref.at[0], glu(vp_ref[...], gp_ref[...]), glu(v_ref[...], g_ref[...]),
              glu(vn_ref[...], gn_ref[...]), first, last, ts)
    ext = ts + 2 * HALO
    for k in range(1, SUBLANES):
        shift_ref[k, 0:ext - SUBLANES, :] = shift_ref[0, k:k + ext - SUBLANES, :]
    pad = (taps - 1) // 2
    width = y_ref.shape[1]

    def lane_chunk(ci, carry):
        c0 = pl.multiple_of(ci * cc, cc)
        for r0 in range(0, ts, rc):
            acc = jnp.zeros((rc, cc), F32)
            for t in range(taps):
                k = (HALO - pad + t) % SUBLANES
                row = r0 + HALO - pad + t - k
                acc = acc + w_ref[t:t + 1, pl.ds(c0, cc)] * shift_ref[k, row:row + rc, pl.ds(c0, cc)]
            y_ref[r0:r0 + rc, pl.ds(c0, cc)] = acc + b_ref[:, pl.ds(c0, cc)]
        return carry

    lax.fori_loop(0, width // cc, lane_chunk, 0)
    y = y_ref[...]
    mu = jnp.mean(y, axis=-1, keepdims=True)
    yc = y - mu
    var = jnp.mean(yc * yc, axis=-1, keepdims=True)
    z = yc * lax.rsqrt(var + EPS) * lg_ref[...] + lb_ref[...]
    o_ref[...] = (z * jax.nn.sigmoid(z)).astype(o_ref.dtype)


def conformer(proj, v_block, g_block, width, conv_w, conv_b, ln_g, ln_b, seq, *, ts=256):
    m = proj.shape[0]
    taps = conv_w.shape[0]
    assert (taps - 1) // 2 <= HALO
    ts = _tile(seq, ts, HALO)
    rc = _tile(ts, 64, 8)
    cc = _tile(width, 256, 128)
    vec = pl.BlockSpec((1, width), lambda i: (0, 0))
    vmem = (2 * (2 * ts * width * 2 + ts * width * 2) + (SUBLANES + 6) * (ts + 2 * HALO) * width * 4
            + (4 << 20))
    return pl.pallas_call(
        functools.partial(_conformer_kernel, ts=ts, tiles_per_seq=seq // ts, taps=taps, rc=rc, cc=cc),
        grid=(m // ts,),
        in_specs=(_halo_specs(ts, width, v_block, m, 1) + _halo_specs(ts, width, g_block, m, 1)
                  + [pl.BlockSpec((taps, width), lambda i: (0, 0)), vec, vec, vec]),
        out_specs=pl.BlockSpec((ts, width), lambda i: (i, 0)),
        out_shape=jax.ShapeDtypeStruct((m, width), BF16),
        scratch_shapes=[pltpu.VMEM((SUBLANES, ts + 2 * HALO, width), F32),
                        pltpu.VMEM((ts, width), F32)],
        compiler_params=_params(("parallel",), vmem),
        name="conformer",
    )(proj, proj, proj, proj, proj, proj, conv_w, conv_b.reshape(1, width),
      ln_g.reshape(1, width), ln_b.reshape(1, width))


def _t5_bucket(rel):
    nb = N_BUCKETS // 2
    max_exact = nb // 2
    ret = jnp.where(rel > 0, nb, 0)
    n = jnp.abs(rel)
    nf = jnp.maximum(n, max_exact).astype(F32)
    large = max_exact + (jnp.log(nf / max_exact) / math.log(MAX_DISTANCE / max_exact)
                         * (nb - max_exact)).astype(jnp.int32)
    large = jnp.minimum(large, nb - 1)
    return ret + jnp.where(n < max_exact, n, large)


def _bias_tiles(rel_bias, t):
    assert t >= MAX_DISTANCE
    h = rel_bias.shape[1]
    wrapped = jnp.arange(2 * t, dtype=jnp.int32)
    wrapped = jnp.where(wrapped < t, wrapped, wrapped - 2 * t)
    rel = jnp.arange(-1, 2, dtype=jnp.int32)[:, None] * t - wrapped[None, :]
    w = jnp.transpose(rel_bias[_t5_bucket(rel)], (2, 0, 1))
    near = jnp.tile(w, (1, 1, t))[:, :, :t * (2 * t - 1)].reshape(h, 3, t, 2 * t - 1)[..., :t]
    far = rel_bias[_t5_bucket(jnp.array([-2 * t, 2 * t], jnp.int32))]
    left = jnp.broadcast_to(far[0][:, None, None, None], (h, 1, t, t))
    right = jnp.broadcast_to(far[1][:, None, None, None], (h, 1, t, t))
    return (jnp.concatenate([left, near, right], axis=1) * LOG2E).astype(F32)


def _attn_kernel(q_ref, k_ref, v_ref, bias_ref, lq1_ref, lk1_ref, lq2_ref, lk2_ref, g_ref, o_ref,
                 sa_ref, sb_ref, p_ref, vt_ref, bt_ref, m_ref, l_ref, acc_ref,
                 *, tq, tk, n_kv, lam_init, rs):
    i = pl.program_id(2)
    d = HEAD_DIM
    nt = (((1,), (1,)), ((), ()))
    ratio = tq // tk
    nkb = tk // MAX_DISTANCE

    @pl.when(i == 0)
    def _():
        for j in range(n_kv):
            vt_ref[j] = v_ref[j * tk:(j + 1) * tk, :].T
        for c in range(ratio + 4):
            for kb in range(nkb):
                for qb in range(tq // MAX_DISTANCE):
                    blk = min(max(kb - qb + nkb * (c - 2), -2), 2) + 2
                    bt_ref[c, kb * MAX_DISTANCE:(kb + 1) * MAX_DISTANCE,
                           qb * MAX_DISTANCE:(qb + 1) * MAX_DISTANCE] = bias_ref[blk]

    m_ref[...] = jnp.full(m_ref.shape, -1e30, F32)
    l_ref[...] = jnp.zeros(l_ref.shape, F32)
    acc_ref[...] = jnp.zeros(acc_ref.shape, F32)

    def logits(j, s_ref):
        r0 = j * tk if isinstance(j, int) else pl.multiple_of(j * tk, tk)
        c = jnp.clip(j - ratio * i, -2, ratio + 1) + 2
        for mp in range(2):
            s_ref[mp] = lax.dot_general(k_ref[pl.ds(r0, tk), mp * d:(mp + 1) * d],
                                        q_ref[:, mp * d:(mp + 1) * d],
                                        nt, preferred_element_type=F32) + bt_ref[c]

    def softmax_pv(j, s_ref):
        for mp in range(2):
            mx = m_ref[mp]
            mn = jnp.maximum(mx, jnp.max(s_ref[mp], axis=0, keepdims=True))
            m_ref[mp] = mn
            a = jnp.exp2(mx - mn)
            lsum = a * l_ref[mp]
            for r in range(0, tk, rs):
                p = jnp.exp2(s_ref[mp, r:r + rs, :] - pltpu.repeat(mn, rs // SUBLANES, axis=0))
                for r8 in range(0, rs, SUBLANES):
                    lsum = lsum + p[r8:r8 + SUBLANES, :]
                p_ref[mp, r:r + rs, :] = p.astype(p_ref.dtype)
            l_ref[mp] = lsum
            acc_ref[mp] = (pltpu.repeat(a, 2 * d // SUBLANES, axis=0) * acc_ref[mp]
                           + jnp.dot(vt_ref[j], p_ref[mp], preferred_element_type=F32))

    logits(0, sa_ref)

    def two_steps(u, carry):
        j = 2 * u
        logits(j + 1, sb_ref)
        softmax_pv(j, sa_ref)
        logits(j + 2, sa_ref)
        softmax_pv(j + 1, sb_ref)
        return carry

    lax.fori_loop(0, n_kv // 2 - 1, two_steps, 0)
    logits(n_kv - 1, sb_ref)
    softmax_pv(n_kv - 2, sa_ref)
    softmax_pv(n_kv - 1, sb_ref)
    lam = (jnp.exp(jnp.sum(lq1_ref[...] * lk1_ref[...], axis=-1, keepdims=True))
           - jnp.exp(jnp.sum(lq2_ref[...] * lk2_ref[...], axis=-1, keepdims=True)) + lam_init)
    l0 = jnp.sum(l_ref[0], axis=0, keepdims=True)
    l1 = jnp.sum(l_ref[1], axis=0, keepdims=True)
    o = acc_ref[0] / l0 - lam * (acc_ref[1] / l1)
    o = o * lax.rsqrt(jnp.mean(o * o, axis=0, keepdims=True) + EPS) * g_ref[...] * (1.0 - lam_init)
    o_ref[...] = o.T.astype(o_ref.dtype)


def diff_attention(proj, rel_bias, lq1, lk1, lq2, lk2, subln_g, batch, seq, n_heads, lam_init, *,
                   tq=1024, tk=512, rs=16):
    m = proj.shape[0]
    d2 = 2 * HEAD_DIM
    tk = _tile(seq, tk, MAX_DISTANCE)
    tq = _tile(seq, tq, tk)
    nq, n_kv = seq // tq, seq // tk
    assert n_kv % 2 == 0, "the key-tile loop is unrolled by two"
    assert tk % MAX_DISTANCE == 0 and tq % tk == 0
    n_bias = tq // tk + 4
    bias = _bias_tiles(rel_bias, MAX_DISTANCE)
    vec = pl.BlockSpec((1, HEAD_DIM), lambda b, h, i: (0, 0))
    vmem = (2 * (2 * seq * d2 * 2 + 2 * tq * d2 * 2) + n_bias * tk * tq * 4 + 2 * tk * tq * 10
            + 2 * tq * d2 * 4 + seq * d2 * 2 + (8 << 20))
    return pl.pallas_call(
        functools.partial(_attn_kernel, tq=tq, tk=tk, n_kv=n_kv, lam_init=lam_init, rs=rs),
        grid=(batch, n_heads, nq),
        scratch_shapes=[pltpu.VMEM((2, tk, tq), F32), pltpu.VMEM((2, tk, tq), F32),
                        pltpu.VMEM((2, tk, tq), BF16), pltpu.VMEM((n_kv, d2, tk), BF16),
                        pltpu.VMEM((n_bias, tk, tq), F32),
                        pltpu.VMEM((2, SUBLANES, tq), F32), pltpu.VMEM((2, SUBLANES, tq), F32),
                        pltpu.VMEM((2, d2, tq), F32)],
        in_specs=[pl.BlockSpec((tq, d2), lambda b, h, i: (b * nq + i, h)),
                  pl.BlockSpec((seq, d2), lambda b, h, i: (b, n_heads + h)),
                  pl.BlockSpec((seq, d2), lambda b, h, i: (b, 2 * n_heads + h)),
                  pl.BlockSpec((None, 5, MAX_DISTANCE, MAX_DISTANCE), lambda b, h, i: (h, 0, 0, 0)),
                  vec, vec, vec, vec,
                  pl.BlockSpec((d2, 1), lambda b, h, i: (0, 0))],
        out_specs=pl.BlockSpec((tq, d2), lambda b, h, i: (b * nq + i, h)),
        out_shape=jax.ShapeDtypeStruct((m, n_heads * d2), BF16),
        compiler_params=_params(("parallel", "parallel", "arbitrary"), vmem),
        name="diff_attention",
    )(proj, proj, proj, bias, lq1.reshape(1, -1), lk1.reshape(1, -1), lq2.reshape(1, -1),
      lk2.reshape(1, -1), subln_g.reshape(d2, 1))


def kernel(x, rel_bias, ev_w_in, ev_lambda_q1, ev_lambda_k1, ev_lambda_q2, ev_lambda_k2, ev_subln_g, ev_conf_w, ev_conf_b, ev_conf_ln_g, ev_conf_ln_b, ev_w_out, od_w_in, od_conv_w, od_w_out, ffn_w_gate, ffn_w_up, ffn_conv_w, ffn_conv_b, ffn_w_down, pre_mix_g, post_mix_g, pre_ffn_g, post_ffn_g):
    batch, seq, d = x.shape
    depth = pre_mix_g.shape[0]
    d2 = 2 * HEAD_DIM
    attn_width = d // 2
    n_heads = attn_width // d2
    qk_width = n_heads * d2
    conv_width = d - attn_width
    assert qk_width == attn_width == conv_width
    stacks = dict(ev_w_in=ev_w_in, ev_w_out=ev_w_out, od_w_in=od_w_in, od_w_out=od_w_out,
                  gate=ffn_w_gate, up=ffn_w_up, down=ffn_w_down)
    ready = {}

    def weight(name, layer):
        if (name, layer) not in ready:
            ready[(name, layer)] = cast_layer(stacks[name], layer)
        return ready.pop((name, layer))

    def hosting(wanted, call, *args, **kw):
        out, cast = call(*args, side_casts=[(stacks[name], layer) for name, layer in wanted], **kw)
        ready.update(zip(wanted, cast))
        return out

    xr = x.reshape(batch * seq, d)
    hn = norm_cast(xr, pre_mix_g[0])
    for i in range(depth):
        j = i // 2
        next_odd = i + 1 < depth and (i + 1) % 2 == 1
        next_even = i + 1 < depth and (i + 1) % 2 == 0
        if i % 2 == 0:
            lam_init = 0.8 - 0.6 * math.exp(-0.3 * i)
            q_scale = jnp.concatenate([jnp.full((qk_width,), HEAD_DIM ** -0.5 * LOG2E, F32),
                                       jnp.ones((ev_w_in.shape[2] - qk_width,), F32)])
            proj = hosting([("ev_w_out", j), ("gate", i), ("up", i)], matmul, [hn], weight("ev_w_in", j), BF16,
                           col_scale=q_scale)
            attn = diff_attention(proj, rel_bias, ev_lambda_q1[j], ev_lambda_k1[j], ev_lambda_q2[j],
                                  ev_lambda_k2[j], ev_subln_g[j], batch, seq, n_heads, lam_init)
            u = conformer(proj, 3, 4, conv_width, ev_conf_w[j], ev_conf_b[j], ev_conf_ln_g[j],
                          ev_conf_ln_b[j], seq)
            mix = matmul([attn, u], weight("ev_w_out", j), BF16)
        else:
            y = short_mixer(hn, weight("od_w_in", j), od_conv_w[j], seq)
            mix = matmul([y], weight("od_w_out", j), BF16)
        xr, hn = residual_norm(xr, mix, post_mix_g[i], pre_ffn_g[i])
        later = [("down", i)]
        if next_odd:
            k = (i + 1) // 2
            later += [("od_w_in", k), ("od_w_out", k), ("gate", i + 1), ("up", i + 1)]
        if next_even:
            later += [("ev_w_in", (i + 1) // 2)]
        h = hosting(later, ffn_up, hn, weight("gate", i), weight("up", i), ffn_conv_w[i], ffn_conv_b[i], seq)
        f = matmul([h], weight("down", i), BF16, tm=512, tn=512)
        xr, hn = residual_norm(xr, f, post_ffn_g[i], pre_mix_g[i + 1] if i + 1 < depth else None)
    return xr.reshape(batch, seq, d)
```
